```python
import math
import jax, jax.numpy as jnp
from jax import lax
import numpy as np

D_MODEL = 1024
BATCH = 8
SEQ = 4096
DEPTH = 2

CHUNK = 64
POOL_WIDTH = D_MODEL // 2
POOL_WINDOWS = (2, 4, 8, 16)
N_POOL_GROUPS = len(POOL_WINDOWS)
POOL_GROUP = POOL_WIDTH // N_POOL_GROUPS
SB_WIDTH = D_MODEL - POOL_WIDTH
SB_HEAD_DIM = 64
SB_HEADS = SB_WIDTH // SB_HEAD_DIM
Q_BLOCK = 128
IN_WIDTH = POOL_WIDTH + 3 * SB_WIDTH
D_FF = 2816
N_EXPERTS = 8
TOP_K = 2
EXPERT_FF = 2816
MOE_BLOCK = 256
N_DENSE = (DEPTH + 1) // 2
N_MOE = DEPTH // 2
DN_ALPHA = (2 * DEPTH) ** 0.25
DN_BETA = (8 * DEPTH) ** -0.25
LN_EPS = 1e-5

kernel_name = "hymba_pool_stickbreak_deepnorm_moe"


def layer_norm(x, g, b):
    xf = x.astype(jnp.float32)
    mu = jnp.mean(xf, axis=-1, keepdims=True)
    xc = xf - mu
    var = jnp.mean(xc * xc, axis=-1, keepdims=True)
    y = xc * lax.rsqrt(var + LN_EPS) * g.astype(jnp.float32) + b.astype(jnp.float32)
    return y.astype(x.dtype)


def pool_mixer(u, pool_w, pool_scale):
    B, S, _ = u.shape
    ug = u.astype(jnp.float32).reshape(B, S, N_POOL_GROUPS, POOL_GROUP)
    cs = jnp.cumsum(ug, axis=1)
    pos = jnp.arange(S)
    outs = []
    for gi, w in enumerate(POOL_WINDOWS):
        c = cs[:, :, gi]
        lagged = jnp.pad(c, ((0, 0), (w, 0), (0, 0)))[:, :S]
        cnt = jnp.minimum(pos + 1, w).astype(jnp.float32)[None, :, None]
        outs.append((c - lagged) / cnt - ug[:, :, gi])
    d = jnp.stack(outs, axis=2)
    y = jnp.einsum('bsgc,gcd->bsgd', d, pool_w.astype(jnp.float32))
    y = y.reshape(B, S, POOL_WIDTH) * pool_scale.astype(jnp.float32)
    return y.astype(u.dtype)


def stick_breaking_attention(q, k, v):
    S = q.shape[2]
    scale = 1.0 / math.sqrt(SB_HEAD_DIM)
    outs = []
    for blk in range(S // Q_BLOCK):
        start = blk * Q_BLOCK
        end = start + Q_BLOCK
        qb = q[:, :, start:end].astype(jnp.float32)
        kb = k[:, :, :end].astype(jnp.float32)
        vb = v[:, :, :end].astype(jnp.float32)
        z = jnp.einsum('bhtd,bhsd->bhts', qb, kb) * scale
        t_idx = start + jnp.arange(Q_BLOCK)[:, None]
        s_idx = jnp.arange(end)[None, :]
        mask = s_idx < t_idx
        log_1m_beta = jnp.where(mask, jax.nn.log_sigmoid(-z), 0.0)
        between = lax.cumsum(log_1m_beta, axis=3, reverse=True) - log_1m_beta
        att = jnp.where(mask, jnp.exp(jax.nn.log_sigmoid(z) + between), 0.0)
        outs.append(jnp.einsum('bhts,bhsd->bhtd', att, vb))
    return jnp.concatenate(outs, axis=2).astype(q.dtype)


def swiglu(x, wg, wu, wd):
    return (jax.nn.silu(x @ wg) * (x @ wu)) @ wd


def moe_ffn(x, router, wg, wu, wd):
    B, S, D = x.shape
    T = B * S
    A = T * TOP_K
    xf = x.reshape(T, D)
    logits = xf.astype(jnp.float32) @ router.astype(jnp.float32)
    top_logits, top_idx = lax.top_k(logits, TOP_K)
    gates = jax.nn.softmax(top_logits, axis=-1)
    flat_e = top_idx.reshape(-1)
    flat_tok = jnp.repeat(jnp.arange(T, dtype=jnp.int32), TOP_K)
    flat_g = gates.reshape(-1)
    order = jnp.argsort(flat_e)
    sorted_e = flat_e[order]
    counts = jnp.zeros((N_EXPERTS,), jnp.int32).at[flat_e].add(1)
    padded = ((counts + MOE_BLOCK - 1) // MOE_BLOCK) * MOE_BLOCK
    start = jnp.cumsum(counts) - counts
    pend = jnp.cumsum(padded)
    pstart = pend - padded
    rank = jnp.arange(A, dtype=jnp.int32) - start[sorted_e]
    dest = pstart[sorted_e] + rank
    n_slots = ((A + MOE_BLOCK - 1) // MOE_BLOCK) * MOE_BLOCK + N_EXPERTS * MOE_BLOCK
    n_blocks = n_slots // MOE_BLOCK
    slot_tok = jnp.zeros((n_slots,), jnp.int32).at[dest].set(flat_tok[order])
    slot_g = jnp.zeros((n_slots,), jnp.float32).at[dest].set(flat_g[order])
    block_e = jnp.minimum(
        jnp.searchsorted(pend, jnp.arange(n_blocks, dtype=jnp.int32) * MOE_BLOCK, side='right'),
        N_EXPERTS - 1)
    xs = xf[slot_tok].reshape(n_blocks, MOE_BLOCK, D)

    def expert_block(args):
        xb, e = args
        return swiglu(xb, wg[e], wu[e], wd[e])

    ys = lax.map(expert_block, (xs, block_e)).reshape(n_slots, D)
    out = jnp.zeros((T, D), jnp.float32).at[slot_tok].add(ys.astype(jnp.float32) * slot_g[:, None])
    return out.astype(x.dtype).reshape(B, S, D)


def setup_inputs(seed: int = 0) -> dict:
    key = jax.random.key(seed)
    ks = jax.random.split(key, 24)
    f32 = jnp.float32
    nrm = lambda k, shape, s: jax.random.normal(k, shape, f32) * s
    x = jax.random.normal(ks[0], (BATCH, SEQ, D_MODEL), f32)
    ln_in_g = 1.0 + nrm(ks[1], (D_MODEL,), 0.02)
    ln_in_b = nrm(ks[2], (D_MODEL,), 0.02)
    w_in = nrm(ks[3], (DEPTH, D_MODEL, IN_WIDTH), D_MODEL ** -0.5)
    w_in = w_in.at[:, :, POOL_WIDTH + 2 * SB_WIDTH:].multiply(DN_BETA)
    pool_w = nrm(ks[4], (DEPTH, N_POOL_GROUPS, POOL_GROUP, POOL_GROUP), POOL_GROUP ** -0.5)
    pool_scale = 1.0 + nrm(ks[5], (DEPTH, POOL_WIDTH), 0.1)
    w_out = nrm(ks[6], (DEPTH, D_MODEL, D_MODEL), D_MODEL ** -0.5) * DN_BETA
    ln1_g = 1.0 + nrm(ks[7], (DEPTH, D_MODEL), 0.02)
    ln1_b = nrm(ks[8], (DEPTH, D_MODEL), 0.02)
    ffn_wg = nrm(ks[9], (N_DENSE, D_MODEL, D_FF), D_MODEL ** -0.5)
    ffn_wu = nrm(ks[10], (N_DENSE, D_MODEL, D_FF), D_MODEL ** -0.5)
    ffn_wd = nrm(ks[11], (N_DENSE, D_FF, D_MODEL), D_FF ** -0.5) * DN_BETA
    moe_router = nrm(ks[12], (N_MOE, D_MODEL, N_EXPERTS), D_MODEL ** -0.5)
    moe_wg = nrm(ks[13], (N_MOE, N_EXPERTS, D_MODEL, EXPERT_FF), D_MODEL ** -0.5)
    moe_wu = nrm(ks[14], (N_MOE, N_EXPERTS, D_MODEL, EXPERT_FF), D_MODEL ** -0.5)
    moe_wd = nrm(ks[15], (N_MOE, N_EXPERTS, EXPERT_FF, D_MODEL), EXPERT_FF ** -0.5) * DN_BETA
    ln2_g = 1.0 + nrm(ks[16], (DEPTH, D_MODEL), 0.02)
    ln2_b = nrm(ks[17], (DEPTH, D_MODEL), 0.02)
    return {"x": x, "ln_in_g": ln_in_g, "ln_in_b": ln_in_b, "w_in": w_in, "pool_w": pool_w,
            "pool_scale": pool_scale, "w_out": w_out, "ln1_g": ln1_g, "ln1_b": ln1_b,
            "ffn_wg": ffn_wg, "ffn_wu": ffn_wu, "ffn_wd": ffn_wd, "moe_router": moe_router,
            "moe_wg": moe_wg, "moe_wu": moe_wu, "moe_wd": moe_wd, "ln2_g": ln2_g, "ln2_b": ln2_b}


def reference(x, ln_in_g, ln_in_b, w_in, pool_w, pool_scale, w_out, ln1_g, ln1_b,
              ffn_wg, ffn_wu, ffn_wd, moe_router, moe_wg, moe_wu, moe_wd, ln2_g, ln2_b):
    B, S, _ = x.shape
    h = layer_norm(x, ln_in_g, ln_in_b)
    for layer in range(DEPTH):
        proj = h @ w_in[layer]
        u_pool = proj[..., :POOL_WIDTH]
        qkv = proj[..., POOL_WIDTH:].reshape(B, S, 3, SB_HEADS, SB_HEAD_DIM)
        qkv = jnp.transpose(qkv, (2, 0, 3, 1, 4))
        y_pool = pool_mixer(u_pool, pool_w[layer], pool_scale[layer])
        y_sb = stick_breaking_attention(qkv[0], qkv[1], qkv[2])
        y_sb = jnp.transpose(y_sb, (0, 2, 1, 3)).reshape(B, S, SB_WIDTH)
        mix = jnp.concatenate([y_pool, y_sb], axis=-1) @ w_out[layer]
        h = layer_norm(DN_ALPHA * h + mix, ln1_g[layer], ln1_b[layer])
        if layer % 2 == 0:
            i = layer // 2
            f = swiglu(h, ffn_wg[i], ffn_wu[i], ffn_wd[i])
        else:
            i = layer // 2
            f = moe_ffn(h, moe_router[i], moe_wg[i], moe_wu[i], moe_wd[i])
        h = layer_norm(DN_ALPHA * h + f, ln2_g[layer], ln2_b[layer])
    return h
```

```python
import functools
import math

import jax
import jax.numpy as jnp
from jax import lax
from jax.experimental import pallas as pl
from jax.experimental.pallas import tpu as pltpu

F32 = jnp.float32
BF16 = jnp.bfloat16

LN_EPS = 1e-5
POOL_WINDOWS = (2, 4, 8, 16)
HEAD_DIM = 64
TOP_K = 2
LANES = 128
POOL_HALO = 16
V7X_VMEM_LIMIT_BYTES = 56 * 1024 * 1024

ATT_TQ = 256
ATT_TK = 256


def _layer_norm(x, g, b):
    mu = jnp.mean(x, axis=-1, keepdims=True)
    xc = x - mu
    var = jnp.mean(xc * xc, axis=-1, keepdims=True)
    return xc * lax.rsqrt(var + LN_EPS) * g + b


def _inproj_kernel(x_ref, lng_ref, lnb_ref, w_ref, pw_ref, ps_ref, *rest, apply_ln, tm, pool_width):
    if apply_ln:
        h_ref, qkv_ref, yp_ref, halo_ref = rest
    else:
        qkv_ref, yp_ref, halo_ref = rest
    si = pl.program_id(1)
    x = x_ref[0]
    if apply_ln:
        h = _layer_norm(x, lng_ref[...], lnb_ref[...])
        h_ref[0] = h
    else:
        h = x
    proj = jnp.dot(h.astype(BF16), w_ref[...], preferred_element_type=F32)
    u = proj[:, :pool_width]

    @pl.when(si == 0)
    def _():
        halo_ref[...] = jnp.zeros_like(halo_ref)

    ext = jnp.concatenate([halo_ref[...], u], axis=0)
    halo_ref[...] = u[tm - POOL_HALO:, :]
    pos = si * tm + lax.broadcasted_iota(jnp.int32, (tm, 1), 0)
    group = pool_width // len(POOL_WINDOWS)
    for gi, w in enumerate(POOL_WINDOWS):
        lo, hi = gi * group, (gi + 1) * group
        s = ext[:, lo:hi]
        shift = 1
        while shift < w:
            s = s + pltpu.roll(s, shift, 0)
            shift *= 2
        cnt = jnp.minimum(pos + 1, w).astype(F32)
        d = s[POOL_HALO:, :] / cnt - u[:, lo:hi]
        y = jnp.dot(d.astype(BF16), pw_ref[gi], preferred_element_type=F32)
        yp_ref[0, :, lo:hi] = (y * ps_ref[:, lo:hi]).astype(BF16)
    sbw = (proj.shape[1] - pool_width) // 3
    q = proj[:, pool_width:pool_width + sbw] * (1.0 / math.sqrt(HEAD_DIM))
    qkv_ref[0, :, :sbw] = q.astype(BF16)
    qkv_ref[0, :, sbw:] = proj[:, pool_width + sbw:].astype(BF16)


def _inproj(x, lng, lnb, w_bf, pw_bf, ps, *, apply_ln, tm=512):
    B, S, D = x.shape
    n_out = w_bf.shape[1]
    pool_width = ps.shape[1]
    qkv_w = n_out - pool_width
    assert S % tm == 0 and tm >= POOL_HALO
    out_shape = [jax.ShapeDtypeStruct((B, S, qkv_w), BF16), jax.ShapeDtypeStruct((B, S, pool_width), BF16)]
    out_specs = [pl.BlockSpec((1, tm, qkv_w), lambda b, s: (b, s, 0)),
                 pl.BlockSpec((1, tm, pool_width), lambda b, s: (b, s, 0))]
    if apply_ln:
        out_shape = [jax.ShapeDtypeStruct((B, S, D), F32)] + out_shape
        out_specs = [pl.BlockSpec((1, tm, D), lambda b, s: (b, s, 0))] + out_specs
    const2 = lambda b, s: (0, 0)
    res = pl.pallas_call(
        functools.partial(_inproj_kernel, apply_ln=apply_ln, tm=tm, pool_width=pool_width),
        grid=(B, S // tm),
        in_specs=[pl.BlockSpec((1, tm, D), lambda b, s: (b, s, 0)),
                  pl.BlockSpec((1, D), const2), pl.BlockSpec((1, D), const2),
                  pl.BlockSpec((D, n_out), const2),
                  pl.BlockSpec(pw_bf.shape, lambda b, s: (0, 0, 0)),
                  pl.BlockSpec((1, pool_width), const2)],
        out_specs=out_specs,
        out_shape=out_shape,
        scratch_shapes=[pltpu.VMEM((POOL_HALO, pool_width), F32)],
        compiler_params=pltpu.CompilerParams(dimension_semantics=("arbitrary", "arbitrary"),
                                             vmem_limit_bytes=V7X_VMEM_LIMIT_BYTES),
        name="inproj_pool",
    )(x, lng, lnb, w_bf, pw_bf, ps)
    if apply_ln:
        return res[0], res[1], res[2]
    return x, res[0], res[1]


def _attn_kernel(q_ref, k_ref, v_ref, tri_ref, o_ref, acc_ref, run_ref, *, tq, tk):
    qi = pl.program_id(2)
    lane = lax.broadcasted_iota(jnp.int32, (tq, LANES), 1)
    first_head = lane < HEAD_DIM
    q = q_ref[0]
    zero = jnp.zeros_like(q)
    q_heads = (jnp.where(first_head, q, zero), jnp.where(first_head, zero, q))
    acc_ref[...] = jnp.zeros_like(acc_ref)
    run_ref[...] = jnp.zeros_like(run_ref)
    tri = tri_ref[...]

    def block(kstart, causal_mask):
        kb = k_ref[0, pl.ds(kstart, tk), :]
        vb = v_ref[0, pl.ds(kstart, tk), :]
        for h in range(2):
            z = lax.dot_general(q_heads[h], kb, (((1,), (1,)), ((), ())), preferred_element_type=F32)
            sp = jnp.maximum(z, 0.0) + jnp.log(1.0 + jnp.exp(-jnp.abs(z)))
            if causal_mask is not None:
                sp = jnp.where(causal_mask, sp, 0.0)
            sp_hi = sp.astype(BF16)
            sp_lo = (sp - sp_hi.astype(F32)).astype(BF16)
            incl = jnp.dot(jnp.concatenate([sp_hi, sp_lo], axis=1), tri, preferred_element_type=F32)
            run = run_ref[h]
            att = jnp.exp(z - incl - jnp.concatenate([run] * (tk // LANES), axis=1))
            if causal_mask is not None:
                att = jnp.where(causal_mask, att, 0.0)
            acc_ref[h] += jnp.dot(att.astype(BF16), vb, preferred_element_type=F32)
            run_ref[h] = run + jnp.broadcast_to(incl[:, 0:1], (tq, LANES))

    row = lax.broadcasted_iota(jnp.int32, (tq, tk), 0)
    col = lax.broadcasted_iota(jnp.int32, (tq, tk), 1)
    block(pl.multiple_of(qi * tq, tq), col < row)

    def body(it, carry):
        block(pl.multiple_of((qi - 1 - it) * tk, tk), None)
        return carry

    lax.fori_loop(0, qi, body, 0)
    o_ref[0] = jnp.where(first_head, acc_ref[0], acc_ref[1]).astype(o_ref.dtype)


def _attention(qkv, *, tq=ATT_TQ, tk=ATT_TK):
    B, S, W = qkv.shape
    sbw = W // 3
    n_pairs = sbw // LANES
    assert tq == tk and S % tq == 0 and sbw % LANES == 0 and LANES == 2 * HEAD_DIM
    idx = jnp.arange(tk)
    tri1 = (idx[:, None] >= idx[None, :]).astype(BF16)
    tri = jnp.concatenate([tri1, tri1], axis=0)
    return pl.pallas_call(
        functools.partial(_attn_kernel, tq=tq, tk=tk),
        grid=(B, n_pairs, S // tq),
        in_specs=[pl.BlockSpec((1, tq, LANES), lambda b, p, i: (b, i, p)),
                  pl.BlockSpec((1, S, LANES), lambda b, p, i: (b, 0, n_pairs + p)),
                  pl.BlockSpec((1, S, LANES), lambda b, p, i: (b, 0, 2 * n_pairs + p)),
                  pl.BlockSpec((2 * tk, tk), lambda b, p, i: (0, 0))],
        out_specs=pl.BlockSpec((1, tq, LANES), lambda b, p, i: (b, i, p)),
        out_shape=jax.ShapeDtypeStruct((B, S, sbw), BF16),
        scratch_shapes=[pltpu.VMEM((2, tq, LANES), F32), pltpu.VMEM((2, tq, LANES), F32)],
        compiler_params=pltpu.CompilerParams(dimension_semantics=("arbitrary", "arbitrary", "arbitrary"),
                                             vmem_limit_bytes=V7X_VMEM_LIMIT_BYTES),
        name="stickbreak_attn",
    )(qkv, qkv, qkv, tri)


def _post_kernel(h_ref, yp_ref, ys_ref, wo_ref, g_ref, b_ref, *rest, alpha, n_experts):
    if n_experts:
        rhi_ref, rlo_ref, h1_ref, route_ref = rest
    else:
        (h1_ref,) = rest
    pw = yp_ref.shape[1]
    mix = (jnp.dot(yp_ref[...], wo_ref[:pw, :], preferred_element_type=F32)
           + jnp.dot(ys_ref[...], wo_ref[pw:, :], preferred_element_type=F32))
    h1 = _layer_norm(alpha * h_ref[...] + mix, g_ref[...], b_ref[...])
    h1_ref[...] = h1
    if n_experts:
        x_hi = h1.astype(BF16)
        x_lo = (h1 - x_hi.astype(F32)).astype(BF16)
        logits = (jnp.dot(x_hi, rhi_ref[...], preferred_element_type=F32)
                  + jnp.dot(x_hi, rlo_ref[...], preferred_element_type=F32)
                  + jnp.dot(x_lo, rhi_ref[...], preferred_element_type=F32))
        lane = lax.broadcasted_iota(jnp.int32, logits.shape, 1)
        neg = jnp.float32(-jnp.inf)
        l1 = jnp.where(lane < n_experts, logits, neg)
        m1 = jnp.max(l1, axis=1, keepdims=True)
        i1 = jnp.min(jnp.where(l1 == m1, lane, LANES), axis=1, keepdims=True)
        l2 = jnp.where(lane == i1, neg, l1)
        m2 = jnp.max(l2, axis=1, keepdims=True)
        i2 = jnp.min(jnp.where(l2 == m2, lane, LANES), axis=1, keepdims=True)
        e2 = jnp.exp(m2 - m1)
        den = 1.0 + e2
        g1 = 1.0 / den
        g2 = e2 / den
        route_ref[...] = jnp.where(lane == i1, g1, 0.0) + jnp.where(lane == i2, g2, 0.0)


def _post(h, yp, ys, wo_bf, g, b, router_split, *, alpha, tm=512):
    T, D = h.shape
    pw, sw = yp.shape[1], ys.shape[1]
    n_experts = 0 if router_split is None else router_split[2]
    const2 = lambda i: (0, 0)
    in_specs = [pl.BlockSpec((tm, D), lambda i: (i, 0)),
                pl.BlockSpec((tm, pw), lambda i: (i, 0)),
                pl.BlockSpec((tm, sw), lambda i: (i, 0)),
                pl.BlockSpec((pw + sw, D), const2),
                pl.BlockSpec((1, D), const2), pl.BlockSpec((1, D), const2)]
    args = [h, yp, ys, wo_bf, g, b]
    out_shape = [jax.ShapeDtypeStruct((T, D), F32)]
    out_specs = [pl.BlockSpec((tm, D), lambda i: (i, 0))]
    if n_experts:
        in_specs += [pl.BlockSpec((D, LANES), const2), pl.BlockSpec((D, LANES), const2)]
        args += [router_split[0], router_split[1]]
        out_shape.append(jax.ShapeDtypeStruct((T, LANES), F32))
        out_specs.append(pl.BlockSpec((tm, LANES), lambda i: (i, 0)))
    res = pl.pallas_call(
        functools.partial(_post_kernel, alpha=alpha, n_experts=n_experts),
        grid=(T // tm,),
        in_specs=in_specs, out_specs=out_specs, out_shape=out_shape,
        compiler_params=pltpu.CompilerParams(dimension_semantics=("arbitrary",),
                                             vmem_limit_bytes=V7X_VMEM_LIMIT_BYTES),
        name="outproj_ln",
    )(*args)
    return (res[0], res[1]) if n_experts else (res[0], None)


def _ffn_kernel(h1_ref, *rest, alpha, fc, gated):
    if gated:
        route_ref, wg_ref, wu_ref, wd_ref, g_ref, b_ref, out_ref, acc_ref, xb_ref = rest
    else:
        wg_ref, wu_ref, wd_ref, g_ref, b_ref, out_ref, acc_ref, xb_ref = rest
    e = pl.program_id(1)

    @pl.when(e == 0)
    def _():
        acc_ref[...] = jnp.zeros_like(acc_ref)
        xb_ref[...] = h1_ref[...].astype(BF16)

    x = xb_ref[...]
    ff = wg_ref.shape[2]
    y = None
    for c in range(ff // fc):
        gcol = jnp.dot(x, wg_ref[0, :, c * fc:(c + 1) * fc], preferred_element_type=F32)
        ucol = jnp.dot(x, wu_ref[0, :, c * fc:(c + 1) * fc], preferred_element_type=F32)
        a = (gcol * jax.nn.sigmoid(gcol)) * ucol
        part = jnp.dot(a.astype(BF16), wd_ref[0, c * fc:(c + 1) * fc, :], preferred_element_type=F32)
        y = part if y is None else y + part
    if gated:
        route = route_ref[...]
        lane = lax.broadcasted_iota(jnp.int32, route.shape, 1)
        gate = jnp.sum(jnp.where(lane == e, route, 0.0), axis=1, keepdims=True)
        y = y * gate
    acc_ref[...] += y

    @pl.when(e == pl.num_programs(1) - 1)
    def _():
        out_ref[...] = _layer_norm(alpha * h1_ref[...] + acc_ref[...], g_ref[...], b_ref[...])


def _ffn(h1, route, wg_bf, wu_bf, wd_bf, g, b, *, alpha, tm=512, fc=256):
    T, D = h1.shape
    E, _, FF = wg_bf.shape
    gated = route is not None
    assert FF % fc == 0 and T % tm == 0
    const2 = lambda i, e: (0, 0)
    in_specs = [pl.BlockSpec((tm, D), lambda i, e: (i, 0))]
    args = [h1]
    if gated:
        in_specs.append(pl.BlockSpec((tm, LANES), lambda i, e: (i, 0)))
        args.append(route)
    in_specs += [pl.BlockSpec((1, D, FF), lambda i, e: (e, 0, 0)),
                 pl.BlockSpec((1, D, FF), lambda i, e: (e, 0, 0)),
                 pl.BlockSpec((1, FF, D), lambda i, e: (e, 0, 0)),
                 pl.BlockSpec((1, D), const2), pl.BlockSpec((1, D), const2)]
    args += [wg_bf, wu_bf, wd_bf, g, b]
    return pl.pallas_call(
        functools.partial(_ffn_kernel, alpha=alpha, fc=fc, gated=gated),
        grid=(T // tm, E),
        in_specs=in_specs,
        out_specs=pl.BlockSpec((tm, D), lambda i, e: (i, 0)),
        out_shape=jax.ShapeDtypeStruct((T, D), F32),
        scratch_shapes=[pltpu.VMEM((tm, D), F32), pltpu.VMEM((tm, D), BF16)],
        compiler_params=pltpu.CompilerParams(dimension_semantics=("arbitrary", "arbitrary"),
                                             vmem_limit_bytes=V7X_VMEM_LIMIT_BYTES),
        name="swiglu_ffn",
    )(*args)


def _split_router(router):
    D, E = router.shape
    r = jnp.zeros((D, LANES), F32).at[:, :E].set(router.astype(F32))
    hi = r.astype(BF16)
    lo = (r - hi.astype(F32)).astype(BF16)
    return hi, lo, E


def kernel(x, ln_in_g, ln_in_b, w_in, pool_w, pool_scale, w_out, ln1_g, ln1_b, ffn_wg, ffn_wu, ffn_wd,
           moe_router, moe_wg, moe_wu, moe_wd, ln2_g, ln2_b):
    B, S, D = x.shape
    depth = w_in.shape[0]
    T = B * S
    alpha = float((2 * depth) ** 0.25)
    row = lambda v: v.reshape(1, -1).astype(F32)
    h = x.astype(F32)
    for layer in range(depth):
        h, qkv, yp = _inproj(h, row(ln_in_g), row(ln_in_b), w_in[layer].astype(BF16),
                             pool_w[layer].astype(BF16), row(pool_scale[layer]), apply_ln=(layer == 0))
        ys = _attention(qkv)
        i = layer // 2
        router_split = _split_router(moe_router[i]) if layer % 2 else None
        h1, route = _post(h.reshape(T, D), yp.reshape(T, -1), ys.reshape(T, -1), w_out[layer].astype(BF16),
                          row(ln1_g[layer]), row(ln1_b[layer]), router_split, alpha=alpha)
        if layer % 2 == 0:
            wg, wu, wd = ffn_wg[i][None], ffn_wu[i][None], ffn_wd[i][None]
        else:
            wg, wu, wd = moe_wg[i], moe_wu[i], moe_wd[i]
        h = _ffn(h1, route, wg.astype(BF16), wu.astype(BF16), wd.astype(BF16),
                 row(ln2_g[layer]), row(ln2_b[layer]), alpha=alpha).reshape(B, S, D)
    return h.astype(x.dtype)
```

```python
import functools
import math

import jax
import jax.numpy as jnp
from jax import lax
from jax.experimental import pallas as pl
from jax.experimental.pallas import tpu as pltpu

F32 = jnp.float32
BF16 = jnp.bfloat16

LN_EPS = 1e-5
POOL_WINDOWS = (2, 4, 8, 16)
HEAD_DIM = 64
TOP_K = 2
LANES = 128
POOL_HALO = 16
V7X_VMEM_LIMIT_BYTES = 56 * 1024 * 1024

ATT_TK = 256
ATT_TQ = 2 * ATT_TK
MASKED_SCORE = -1e30


def _layer_norm(x, g, b):
    mu = jnp.mean(x, axis=-1, keepdims=True)
    xc = x - mu
    var = jnp.mean(xc * xc, axis=-1, keepdims=True)
    return xc * lax.rsqrt(var + LN_EPS) * g + b


def _inproj_kernel(x_ref, lng_ref, lnb_ref, w_ref, pw_ref, ps_ref, *rest, apply_ln, tm, pool_width):
    if apply_ln:
        h_ref, qkv_ref, yp_ref, halo_ref = rest
    else:
        qkv_ref, yp_ref, halo_ref = rest
    si = pl.program_id(1)
    x = x_ref[0]
    if apply_ln:
        h = _layer_norm(x, lng_ref[...], lnb_ref[...])
        h_ref[0] = h
    else:
        h = x
    proj = jnp.dot(h.astype(BF16), w_ref[...], preferred_element_type=F32)
    u = proj[:, :pool_width]

    @pl.when(si == 0)
    def _():
        halo_ref[...] = jnp.zeros_like(halo_ref)

    ext = jnp.concatenate([halo_ref[...], u], axis=0)
    halo_ref[...] = u[tm - POOL_HALO:, :]
    pos = si * tm + lax.broadcasted_iota(jnp.int32, (tm, 1), 0)
    group = pool_width // len(POOL_WINDOWS)
    for gi, w in enumerate(POOL_WINDOWS):
        lo, hi = gi * group, (gi + 1) * group
        s = ext[:, lo:hi]
        shift = 1
        while shift < w:
            s = s + pltpu.roll(s, shift, 0)
            shift *= 2
        cnt = jnp.minimum(pos + 1, w).astype(F32)
        d = s[POOL_HALO:, :] / cnt - u[:, lo:hi]
        y = jnp.dot(d.astype(BF16), pw_ref[gi], preferred_element_type=F32)
        yp_ref[0, :, lo:hi] = (y * ps_ref[:, lo:hi]).astype(BF16)
    sbw = (proj.shape[1] - pool_width) // 3
    q = proj[:, pool_width:pool_width + sbw] * (1.0 / math.sqrt(HEAD_DIM))
    qkv_ref[0, :, :sbw] = q.astype(BF16)
    qkv_ref[0, :, sbw:] = proj[:, pool_width + sbw:].astype(BF16)


def _inproj(x, lng, lnb, w_bf, pw_bf, ps, *, apply_ln, tm=512):
    B, S, D = x.shape
    n_out = w_bf.shape[1]
    pool_width = ps.shape[1]
    qkv_w = n_out - pool_width
    assert S % tm == 0 and tm >= POOL_HALO
    out_shape = [jax.ShapeDtypeStruct((B, S, qkv_w), BF16), jax.ShapeDtypeStruct((B, S, pool_width), BF16)]
    out_specs = [pl.BlockSpec((1, tm, qkv_w), lambda b, s: (b, s, 0)),
                 pl.BlockSpec((1, tm, pool_width), lambda b, s: (b, s, 0))]
    if apply_ln:
        out_shape = [jax.ShapeDtypeStruct((B, S, D), F32)] + out_shape
        out_specs = [pl.BlockSpec((1, tm, D), lambda b, s: (b, s, 0))] + out_specs
    const2 = lambda b, s: (0, 0)
    res = pl.pallas_call(
        functools.partial(_inproj_kernel, apply_ln=apply_ln, tm=tm, pool_width=pool_width),
        grid=(B, S // tm),
        in_specs=[pl.BlockSpec((1, tm, D), lambda b, s: (b, s, 0)),
                  pl.BlockSpec((1, D), const2), pl.BlockSpec((1, D), const2),
                  pl.BlockSpec((D, n_out), const2),
                  pl.BlockSpec(pw_bf.shape, lambda b, s: (0, 0, 0)),
                  pl.BlockSpec((1, pool_width), const2)],
        out_specs=out_specs,
        out_shape=out_shape,
        scratch_shapes=[pltpu.VMEM((POOL_HALO, pool_width), F32)],
        compiler_params=pltpu.CompilerParams(dimension_semantics=("arbitrary", "arbitrary"),
                                             vmem_limit_bytes=V7X_VMEM_LIMIT_BYTES),
        name="inproj_pool",
    )(x, lng, lnb, w_bf, pw_bf, ps)
    if apply_ln:
        return res[0], res[1], res[2]
    return x, res[0], res[1]


def _attn_kernel(q_ref, k_ref, v_ref, tri_ref, o_ref, acc_ref, run_ref, *, tq, tk):
    qi = pl.program_id(2)
    lane = lax.broadcasted_iota(jnp.int32, (tq, LANES), 1)
    first_head = lane < HEAD_DIM
    q = q_ref[0]
    zero = jnp.zeros_like(q)
    q_heads = (jnp.where(first_head, q, zero), jnp.where(first_head, zero, q))
    acc_ref[...] = jnp.zeros_like(acc_ref)
    run_ref[...] = jnp.zeros_like(run_ref)
    tri = tri_ref[...]
    sign_bit = jnp.uint32(0x80000000)

    def scores(h, kstart, causal_mask):
        kb = k_ref[0, pl.ds(kstart, tk), :]
        z = lax.dot_general(q_heads[h], kb, (((1,), (1,)), ((), ())), preferred_element_type=F32)
        if causal_mask is not None:
            z = jnp.where(causal_mask, z, MASKED_SCORE)
        neg_abs = lax.bitcast_convert_type(lax.bitcast_convert_type(z, jnp.uint32) | sign_bit, F32)
        sp = jnp.maximum(z, 0.0) + jnp.log(1.0 + jnp.exp(neg_abs))
        later = jnp.dot(sp.astype(BF16), tri, preferred_element_type=F32)
        return z - sp, later, later[:, 0:1] + sp[:, 0:1]

    def span(kstart, masks):
        nblk = len(masks)
        wide = lambda r: jnp.concatenate([r] * (tk // LANES), axis=1)
        for h in range(2):
            parts = [scores(h, kstart + b * tk, masks[b]) for b in range(nblk)]
            run = run_ref[h]
            att = [None] * nblk
            for b in reversed(range(nblk)):
                log_beta, later, total = parts[b]
                att[b] = jnp.exp(log_beta - later - wide(run))
                run = run + jnp.broadcast_to(total, (tq, LANES))
            run_ref[h] = run
            acc_ref[h] += jnp.dot(jnp.concatenate(att, axis=1).astype(BF16),
                                  v_ref[0, pl.ds(kstart, nblk * tk), :], preferred_element_type=F32)

    lead = (lax.broadcasted_iota(jnp.int32, (tq, tk), 0) - lax.broadcasted_iota(jnp.int32, (tq, tk), 1))
    diag_masks = (lead > 0, lead > tk)
    odd = qi % 2

    @pl.when(odd == 0)
    def _():
        span(pl.multiple_of(qi * tq, tq), diag_masks)

    @pl.when(odd == 1)
    def _():
        span(pl.multiple_of((qi - 1) * tq, tq), (None, None) + diag_masks)

    top = qi - 1 - odd

    def body(it, carry):
        span(pl.multiple_of((top - 2 * it - 1) * tq, tq), (None,) * 4)
        return carry

    lax.fori_loop(0, qi // 2, body, 0)
    o_ref[0] = jnp.where(first_head, acc_ref[0], acc_ref[1]).astype(o_ref.dtype)


def _attention(qkv, *, tq=ATT_TQ, tk=ATT_TK):
    B, S, W = qkv.shape
    sbw = W // 3
    n_pairs = sbw // LANES
    assert tq == 2 * tk and S % tq == 0 and sbw % LANES == 0 and LANES == 2 * HEAD_DIM
    idx = jnp.arange(tk)
    tri = (idx[:, None] > idx[None, :]).astype(BF16)
    return pl.pallas_call(
        functools.partial(_attn_kernel, tq=tq, tk=tk),
        grid=(B, n_pairs, S // tq),
        in_specs=[pl.BlockSpec((1, tq, LANES), lambda b, p, i: (b, i, p)),
                  pl.BlockSpec((1, S, LANES), lambda b, p, i: (b, 0, n_pairs + p)),
                  pl.BlockSpec((1, S, LANES), lambda b, p, i: (b, 0, 2 * n_pairs + p)),
                  pl.BlockSpec((tk, tk), lambda b, p, i: (0, 0))],
        out_specs=pl.BlockSpec((1, tq, LANES), lambda b, p, i: (b, i, p)),
        out_shape=jax.ShapeDtypeStruct((B, S, sbw), BF16),
        scratch_shapes=[pltpu.VMEM((2, tq, LANES), F32), pltpu.VMEM((2, tq, LANES), F32)],
        compiler_params=pltpu.CompilerParams(dimension_semantics=("arbitrary", "arbitrary", "arbitrary"),
                                             vmem_limit_bytes=V7X_VMEM_LIMIT_BYTES),
        name="stickbreak_attn",
    )(qkv, qkv, qkv, tri)


def _post_kernel(h_ref, yp_ref, ys_ref, wo_ref, g_ref, b_ref, *rest, alpha, n_experts):
    if n_experts:
        rhi_ref, rlo_ref, h1_ref, route_ref = rest
    else:
        (h1_ref,) = rest
    pw = yp_ref.shape[1]
    mix = (jnp.dot(yp_ref[...], wo_ref[:pw, :], preferred_element_type=F32)
           + jnp.dot(ys_ref[...], wo_ref[pw:, :], preferred_element_type=F32))
    h1 = _layer_norm(alpha * h_ref[...] + mix, g_ref[...], b_ref[...])
    h1_ref[...] = h1
    if n_experts:
        x_hi = h1.astype(BF16)
        x_lo = (h1 - x_hi.astype(F32)).astype(BF16)
        logits = (jnp.dot(x_hi, rhi_ref[...], preferred_element_type=F32)
                  + jnp.dot(x_hi, rlo_ref[...], preferred_element_type=F32)
                  + jnp.dot(x_lo, rhi_ref[...], preferred_element_type=F32))
        lane = lax.broadcasted_iota(jnp.int32, logits.shape, 1)
        neg = jnp.float32(-jnp.inf)
        l1 = jnp.where(lane < n_experts, logits, neg)
        m1 = jnp.max(l1, axis=1, keepdims=True)
        i1 = jnp.min(jnp.where(l1 == m1, lane, LANES), axis=1, keepdims=True)
        l2 = jnp.where(lane == i1, neg, l1)
        m2 = jnp.max(l2, axis=1, keepdims=True)
        i2 = jnp.min(jnp.where(l2 == m2, lane, LANES), axis=1, keepdims=True)
        e2 = jnp.exp(m2 - m1)
        den = 1.0 + e2
        g1 = 1.0 / den
        g2 = e2 / den
        route_ref[...] = jnp.where(lane == i1, g1, 0.0) + jnp.where(lane == i2, g2, 0.0)


def _post(h, yp, ys, wo_bf, g, b, router_split, *, alpha, tm=512):
    T, D = h.shape
    pw, sw = yp.shape[1], ys.shape[1]
    n_experts = 0 if router_split is None else router_split[2]
    const2 = lambda i: (0, 0)
    in_specs = [pl.BlockSpec((tm, D), lambda i: (i, 0)),
                pl.BlockSpec((tm, pw), lambda i: (i, 0)),
                pl.BlockSpec((tm, sw), lambda i: (i, 0)),
                pl.BlockSpec((pw + sw, D), const2),
                pl.BlockSpec((1, D), const2), pl.BlockSpec((1, D), const2)]
    args = [h, yp, ys, wo_bf, g, b]
    out_shape = [jax.ShapeDtypeStruct((T, D), F32)]
    out_specs = [pl.BlockSpec((tm, D), lambda i: (i, 0))]
    if n_experts:
        in_specs += [pl.BlockSpec((D, LANES), const2), pl.BlockSpec((D, LANES), const2)]
        args += [router_split[0], router_split[1]]
        out_shape.append(jax.ShapeDtypeStruct((T, LANES), F32))
        out_specs.append(pl.BlockSpec((tm, LANES), lambda i: (i, 0)))
    res = pl.pallas_call(
        functools.partial(_post_kernel, alpha=alpha, n_experts=n_experts),
        grid=(T // tm,),
        in_specs=in_specs, out_specs=out_specs, out_shape=out_shape,
        compiler_params=pltpu.CompilerParams(dimension_semantics=("arbitrary",),
                                             vmem_limit_bytes=V7X_VMEM_LIMIT_BYTES),
        name="outproj_ln",
    )(*args)
    return (res[0], res[1]) if n_experts else (res[0], None)


def _ffn_kernel(h1_ref, *rest, alpha, fc, gated):
    if gated:
        route_ref, wg_ref, wu_ref, wd_ref, g_ref, b_ref, out_ref, acc_ref, xb_ref = rest
    else:
        wg_ref, wu_ref, wd_ref, g_ref, b_ref, out_ref, acc_ref, xb_ref = rest
    e = pl.program_id(1)

    @pl.when(e == 0)
    def _():
        acc_ref[...] = jnp.zeros_like(acc_ref)
        xb_ref[...] = h1_ref[...].astype(BF16)

    x = xb_ref[...]
    ff = wg_ref.shape[2]
    y = None
    for c in range(ff // fc):
        gcol = jnp.dot(x, wg_ref[0, :, c * fc:(c + 1) * fc], preferred_element_type=F32)
        ucol = jnp.dot(x, wu_ref[0, :, c * fc:(c + 1) * fc], preferred_element_type=F32)
        a = (gcol * jax.nn.sigmoid(gcol)) * ucol
        part = jnp.dot(a.astype(BF16), wd_ref[0, c * fc:(c + 1) * fc, :], preferred_element_type=F32)
        y = part if y is None else y + part
    if gated:
        route = route_ref[...]
        lane = lax.broadcasted_iota(jnp.int32, route.shape, 1)
        gate = jnp.sum(jnp.where(lane == e, route, 0.0), axis=1, keepdims=True)
        y = y * gate
    acc_ref[...] += y

    @pl.when(e == pl.num_programs(1) - 1)
    def _():
        out_ref[...] = _layer_norm(alpha * h1_ref[...] + acc_ref[...], g_ref[...], b_ref[...])


def _ffn(h1, route, wg_bf, wu_bf, wd_bf, g, b, *, alpha, tm=512, fc=256):
    T, D = h1.shape
    E, _, FF = wg_bf.shape
    gated = route is not None
    assert FF % fc == 0 and T % tm == 0
    const2 = lambda i, e: (0, 0)
    in_specs = [pl.BlockSpec((tm, D), lambda i, e: (i, 0))]
    args = [h1]
    if gated:
        in_specs.append(pl.BlockSpec((tm, LANES), lambda i, e: (i, 0)))
        args.append(route)
    in_specs += [pl.BlockSpec((1, D, FF), lambda i, e: (e, 0, 0)),
                 pl.BlockSpec((1, D, FF), lambda i, e: (e, 0, 0)),
                 pl.BlockSpec((1, FF, D), lambda i, e: (e, 0, 0)),
                 pl.BlockSpec((1, D), const2), pl.BlockSpec((1, D), const2)]
    args += [wg_bf, wu_bf, wd_bf, g, b]
    return pl.pallas_call(
        functools.partial(_ffn_kernel, alpha=alpha, fc=fc, gated=gated),
        grid=(T // tm, E),
        in_specs=in_specs,
        out_specs=pl.BlockSpec((tm, D), lambda i, e: (i, 0)),
        out_shape=jax.ShapeDtypeStruct((T, D), F32),
        scratch_shapes=[pltpu.VMEM((tm, D), F32), pltpu.VMEM((tm, D), BF16)],
        compiler_params=pltpu.CompilerParams(dimension_semantics=("arbitrary", "arbitrary"),
                                             vmem_limit_bytes=V7X_VMEM_LIMIT_BYTES),
        name="swiglu_ffn",
    )(*args)


def _split_router(router):
    D, E = router.shape
    r = jnp.zeros((D, LANES), F32).at[:, :E].set(router.astype(F32))
    hi = r.astype(BF16)
    lo = (r - hi.astype(F32)).astype(BF16)
    return hi, lo, E


def kernel(x, ln_in_g, ln_in_b, w_in, pool_w, pool_scale, w_out, ln1_g, ln1_b, ffn_wg, ffn_wu, ffn_wd,
           moe_router, moe_wg, moe_wu, moe_wd, ln2_g, ln2_b):
    B, S, D = x.shape
    depth = w_in.shape[0]
    T = B * S
    alpha = float((2 * depth) ** 0.25)
    row = lambda v: v.reshape(1, -1).astype(F32)
    h = x.astype(F32)
    for layer in range(depth):
        h, qkv, yp = _inproj(h, row(ln_in_g), row(ln_in_b), w_in[layer].astype(BF16),
                             pool_w[layer].astype(BF16), row(pool_scale[layer]), apply_ln=(layer == 0))
        ys = _attention(qkv)
        i = layer // 2
        router_split = _split_router(moe_router[i]) if layer % 2 else None
        h1, route = _post(h.reshape(T, D), yp.reshape(T, -1), ys.reshape(T, -1), w_out[layer].astype(BF16),
                          row(ln1_g[layer]), row(ln1_b[layer]), router_split, alpha=alpha)
        if layer % 2 == 0:
            wg, wu, wd = ffn_wg[i][None], ffn_wu[i][None], ffn_wd[i][None]
        else:
            wg, wu, wd = moe_wg[i], moe_wu[i], moe_wd[i]
        h = _ffn(h1, route, wg.astype(BF16), wu.astype(BF16), wd.astype(BF16),
                 row(ln2_g[layer]), row(ln2_b[layer]), alpha=alpha).reshape(B, S, D)
    return h.astype(x.dtype)
```

```python
import functools
import math

import jax
import jax.numpy as jnp
from jax import lax
from jax.experimental import pallas as pl
from jax.experimental.pallas import tpu as pltpu

F32 = jnp.float32
BF16 = jnp.bfloat16

LN_EPS = 1e-5
POOL_WINDOWS = (2, 4, 8, 16)
HEAD_DIM = 64
TOP_K = 2
LANES = 128
POOL_HALO = 16
V7X_VMEM_LIMIT_BYTES = 56 * 1024 * 1024

ATT_TK = 256
ATT_TQ = 2 * ATT_TK
MASKED_SCORE = -1e30


def _layer_norm(x, g, b):
    mu = jnp.mean(x, axis=-1, keepdims=True)
    xc = x - mu
    var = jnp.mean(xc * xc, axis=-1, keepdims=True)
    return xc * lax.rsqrt(var + LN_EPS) * g + b


def _inproj_kernel(x_ref, lng_ref, lnb_ref, w_ref, pw_ref, ps_ref, *rest, apply_ln, tm, pool_width):
    if apply_ln:
        h_ref, qkv_ref, yp_ref, halo_ref = rest
    else:
        qkv_ref, yp_ref, halo_ref = rest
    si = pl.program_id(1)
    x = x_ref[0]
    if apply_ln:
        h = _layer_norm(x, lng_ref[...], lnb_ref[...])
        h_ref[0] = h
    else:
        h = x
    proj = jnp.dot(h.astype(BF16), w_ref[...], preferred_element_type=F32)
    u = proj[:, :pool_width]

    @pl.when(si == 0)
    def _():
        halo_ref[...] = jnp.zeros_like(halo_ref)

    ext = jnp.concatenate([halo_ref[...], u], axis=0)
    halo_ref[...] = u[tm - POOL_HALO:, :]
    pos = si * tm + lax.broadcasted_iota(jnp.int32, (tm, 1), 0)
    group = pool_width // len(POOL_WINDOWS)
    for gi, w in enumerate(POOL_WINDOWS):
        lo, hi = gi * group, (gi + 1) * group
        s = ext[:, lo:hi]
        shift = 1
        while shift < w:
            s = s + pltpu.roll(s, shift, 0)
            shift *= 2
        cnt = jnp.minimum(pos + 1, w).astype(F32)
        d = s[POOL_HALO:, :] / cnt - u[:, lo:hi]
        y = jnp.dot(d.astype(BF16), pw_ref[gi], preferred_element_type=F32)
        yp_ref[0, :, lo:hi] = (y * ps_ref[:, lo:hi]).astype(BF16)
    sbw = (proj.shape[1] - pool_width) // 3
    q = proj[:, pool_width:pool_width + sbw] * (1.0 / math.sqrt(HEAD_DIM))
    qkv_ref[0, :, :sbw] = q.astype(BF16)
    qkv_ref[0, :, sbw:] = proj[:, pool_width + sbw:].astype(BF16)


def _inproj(x, lng, lnb, w_bf, pw_bf, ps, *, apply_ln, tm=512):
    B, S, D = x.shape
    n_out = w_bf.shape[1]
    pool_width = ps.shape[1]
    qkv_w = n_out - pool_width
    assert S % tm == 0 and tm >= POOL_HALO
    out_shape = [jax.ShapeDtypeStruct((B, S, qkv_w), BF16), jax.ShapeDtypeStruct((B, S, pool_width), BF16)]
    out_specs = [pl.BlockSpec((1, tm, qkv_w), lambda b, s: (b, s, 0)),
                 pl.BlockSpec((1, tm, pool_width), lambda b, s: (b, s, 0))]
    if apply_ln:
        out_shape = [jax.ShapeDtypeStruct((B, S, D), F32)] + out_shape
        out_specs = [pl.BlockSpec((1, tm, D), lambda b, s: (b, s, 0))] + out_specs
    const2 = lambda b, s: (0, 0)
    res = pl.pallas_call(
        functools.partial(_inproj_kernel, apply_ln=apply_ln, tm=tm, pool_width=pool_width),
        grid=(B, S // tm),
        in_specs=[pl.BlockSpec((1, tm, D), lambda b, s: (b, s, 0)),
                  pl.BlockSpec((1, D), const2), pl.BlockSpec((1, D), const2),
                  pl.BlockSpec((D, n_out), const2),
                  pl.BlockSpec(pw_bf.shape, lambda b, s: (0, 0, 0)),
                  pl.BlockSpec((1, pool_width), const2)],
        out_specs=out_specs,
        out_shape=out_shape,
        scratch_shapes=[pltpu.VMEM((POOL_HALO, pool_width), F32)],
        compiler_params=pltpu.CompilerParams(dimension_semantics=("arbitrary", "arbitrary"),
                                             vmem_limit_bytes=V7X_VMEM_LIMIT_BYTES),
        name="inproj_pool",
    )(x, lng, lnb, w_bf, pw_bf, ps)
    if apply_ln:
        return res[0], res[1], res[2]
    return x, res[0], res[1]


def _attn_kernel(q_ref, k_ref, v_ref, tri_ref, o_ref, acc_ref, run_ref, *, tq, tk):
    qi = pl.program_id(2)
    lane = lax.broadcasted_iota(jnp.int32, (tq, LANES), 1)
    first_head = lane < HEAD_DIM
    q = q_ref[0]
    zero = jnp.zeros_like(q)
    q_heads = (jnp.where(first_head, q, zero), jnp.where(first_head, zero, q))
    acc_ref[...] = jnp.zeros_like(acc_ref)
    run_ref[...] = jnp.zeros_like(run_ref)
    tri = tri_ref[...]
    sign_bit = jnp.uint32(0x80000000)

    def scores(h, kstart, causal_mask):
        kb = k_ref[0, pl.ds(kstart, tk), :]
        z = lax.dot_general(q_heads[h], kb, (((1,), (1,)), ((), ())), preferred_element_type=F32)
        if causal_mask is not None:
            z = jnp.where(causal_mask, z, MASKED_SCORE)
        neg_abs = lax.bitcast_convert_type(lax.bitcast_convert_type(z, jnp.uint32) | sign_bit, F32)
        sp = jnp.maximum(z, 0.0) + jnp.log(1.0 + jnp.exp(neg_abs))
        later = jnp.dot(sp.astype(BF16), tri, preferred_element_type=F32)
        return z - sp, later, later[:, 0:1] + sp[:, 0:1]

    def span(kstart, masks):
        nblk = len(masks)
        wide = lambda r: jnp.concatenate([r] * (tk // LANES), axis=1)
        for h in range(2):
            parts = [scores(h, kstart + b * tk, masks[b]) for b in range(nblk)]
            run = run_ref[h]
            att = [None] * nblk
            for b in reversed(range(nblk)):
                log_beta, later, total = parts[b]
                att[b] = jnp.exp(log_beta - later - wide(run))
                run = run + jnp.broadcast_to(total, (tq, LANES))
            run_ref[h] = run
            acc_ref[h] += jnp.dot(jnp.concatenate(att, axis=1).astype(BF16),
                                  v_ref[0, pl.ds(kstart, nblk * tk), :], preferred_element_type=F32)

    lead = (lax.broadcasted_iota(jnp.int32, (tq, tk), 0) - lax.broadcasted_iota(jnp.int32, (tq, tk), 1))
    diag_masks = (lead > 0, lead > tk)
    odd = qi % 2

    @pl.when(odd == 0)
    def _():
        span(pl.multiple_of(qi * tq, tq), diag_masks)

    @pl.when(odd == 1)
    def _():
        span(pl.multiple_of((qi - 1) * tq, tq), (None, None) + diag_masks)

    top = qi - 1 - odd

    def body(it, carry):
        span(pl.multiple_of((top - 2 * it - 1) * tq, tq), (None,) * 4)
        return carry

    lax.fori_loop(0, qi // 2, body, 0)
    o_ref[0] = jnp.where(first_head, acc_ref[0], acc_ref[1]).astype(o_ref.dtype)


def _attention(qkv, *, tq=ATT_TQ, tk=ATT_TK):
    B, S, W = qkv.shape
    sbw = W // 3
    n_pairs = sbw // LANES
    assert tq == 2 * tk and S % tq == 0 and sbw % LANES == 0 and LANES == 2 * HEAD_DIM
    idx = jnp.arange(tk)
    tri = (idx[:, None] > idx[None, :]).astype(BF16)
    return pl.pallas_call(
        functools.partial(_attn_kernel, tq=tq, tk=tk),
        grid=(B, n_pairs, S // tq),
        in_specs=[pl.BlockSpec((1, tq, LANES), lambda b, p, i: (b, i, p)),
                  pl.BlockSpec((1, S, LANES), lambda b, p, i: (b, 0, n_pairs + p)),
                  pl.BlockSpec((1, S, LANES), lambda b, p, i: (b, 0, 2 * n_pairs + p)),
                  pl.BlockSpec((tk, tk), lambda b, p, i: (0, 0))],
        out_specs=pl.BlockSpec((1, tq, LANES), lambda b, p, i: (b, i, p)),
        out_shape=jax.ShapeDtypeStruct((B, S, sbw), BF16),
        scratch_shapes=[pltpu.VMEM((2, tq, LANES), F32), pltpu.VMEM((2, tq, LANES), F32)],
        compiler_params=pltpu.CompilerParams(dimension_semantics=("arbitrary", "arbitrary", "arbitrary"),
                                             vmem_limit_bytes=V7X_VMEM_LIMIT_BYTES),
        name="stickbreak_attn",
    )(qkv, qkv, qkv, tri)


def _post_kernel(h_ref, yp_ref, ys_ref, wo_ref, g_ref, b_ref, *rest, alpha, n_experts):
    if n_experts:
        rhi_ref, rlo_ref, h1_ref, route_ref = rest
    else:
        (h1_ref,) = rest
    pw = yp_ref.shape[1]
    mix = (jnp.dot(yp_ref[...], wo_ref[:pw, :], preferred_element_type=F32)
           + jnp.dot(ys_ref[...], wo_ref[pw:, :], preferred_element_type=F32))
    h1 = _layer_norm(alpha * h_ref[...] + mix, g_ref[...], b_ref[...])
    h1_ref[...] = h1
    if n_experts:
        x_hi = h1.astype(BF16)
        x_lo = (h1 - x_hi.astype(F32)).astype(BF16)
        logits = (jnp.dot(x_hi, rhi_ref[...], preferred_element_type=F32)
                  + jnp.dot(x_hi, rlo_ref[...], preferred_element_type=F32)
                  + jnp.dot(x_lo, rhi_ref[...], preferred_element_type=F32))
        lane = lax.broadcasted_iota(jnp.int32, logits.shape, 1)
        neg = jnp.float32(-jnp.inf)
        l1 = jnp.where(lane < n_experts, logits, neg)
        m1 = jnp.max(l1, axis=1, keepdims=True)
        i1 = jnp.min(jnp.where(l1 == m1, lane, LANES), axis=1, keepdims=True)
        l2 = jnp.where(lane == i1, neg, l1)
        m2 = jnp.max(l2, axis=1, keepdims=True)
        i2 = jnp.min(jnp.where(l2 == m2, lane, LANES), axis=1, keepdims=True)
        e2 = jnp.exp(m2 - m1)
        den = 1.0 + e2
        g1 = 1.0 / den
        g2 = e2 / den
        route_ref[...] = jnp.where(lane == 0, i1.astype(F32),
                                   jnp.where(lane == 1, i2.astype(F32),
                                             jnp.where(lane == 2, g1, jnp.where(lane == 3, g2, 0.0))))


def _post(h, yp, ys, wo_bf, g, b, router_split, *, alpha, tm=512):
    T, D = h.shape
    pw, sw = yp.shape[1], ys.shape[1]
    n_experts = 0 if router_split is None else router_split[2]
    const2 = lambda i: (0, 0)
    in_specs = [pl.BlockSpec((tm, D), lambda i: (i, 0)),
                pl.BlockSpec((tm, pw), lambda i: (i, 0)),
                pl.BlockSpec((tm, sw), lambda i: (i, 0)),
                pl.BlockSpec((pw + sw, D), const2),
                pl.BlockSpec((1, D), const2), pl.BlockSpec((1, D), const2)]
    args = [h, yp, ys, wo_bf, g, b]
    out_shape = [jax.ShapeDtypeStruct((T, D), F32)]
    out_specs = [pl.BlockSpec((tm, D), lambda i: (i, 0))]
    if n_experts:
        in_specs += [pl.BlockSpec((D, LANES), const2), pl.BlockSpec((D, LANES), const2)]
        args += [router_split[0], router_split[1]]
        out_shape.append(jax.ShapeDtypeStruct((T, LANES), F32))
        out_specs.append(pl.BlockSpec((tm, LANES), lambda i: (i, 0)))
    res = pl.pallas_call(
        functools.partial(_post_kernel, alpha=alpha, n_experts=n_experts),
        grid=(T // tm,),
        in_specs=in_specs, out_specs=out_specs, out_shape=out_shape,
        compiler_params=pltpu.CompilerParams(dimension_semantics=("arbitrary",),
                                             vmem_limit_bytes=V7X_VMEM_LIMIT_BYTES),
        name="outproj_ln",
    )(*args)
    return (res[0], res[1]) if n_experts else (res[0], None)


def _swiglu(x_bf, wg_ref, wu_ref, wd_ref, fc):
    ff = wg_ref.shape[2]
    y = None
    for c in range(ff // fc):
        gcol = jnp.dot(x_bf, wg_ref[0, :, c * fc:(c + 1) * fc], preferred_element_type=F32)
        ucol = jnp.dot(x_bf, wu_ref[0, :, c * fc:(c + 1) * fc], preferred_element_type=F32)
        a = (gcol * jax.nn.sigmoid(gcol)) * ucol
        part = jnp.dot(a.astype(BF16), wd_ref[0, c * fc:(c + 1) * fc, :], preferred_element_type=F32)
        y = part if y is None else y + part
    return y


def _ffn_kernel(h1_ref, wg_ref, wu_ref, wd_ref, g_ref, b_ref, out_ref, *, alpha, fc):
    h1 = h1_ref[...]
    y = _swiglu(h1.astype(BF16), wg_ref, wu_ref, wd_ref, fc)
    out_ref[...] = _layer_norm(alpha * h1 + y, g_ref[...], b_ref[...])


def _ffn(h1, wg_bf, wu_bf, wd_bf, g, b, *, alpha, tm=512, fc=256):
    T, D = h1.shape
    _, _, FF = wg_bf.shape
    assert FF % fc == 0 and T % tm == 0
    const2 = lambda i: (0, 0)
    const3 = lambda i: (0, 0, 0)
    return pl.pallas_call(
        functools.partial(_ffn_kernel, alpha=alpha, fc=fc),
        grid=(T // tm,),
        in_specs=[pl.BlockSpec((tm, D), lambda i: (i, 0)),
                  pl.BlockSpec((1, D, FF), const3), pl.BlockSpec((1, D, FF), const3),
                  pl.BlockSpec((1, FF, D), const3),
                  pl.BlockSpec((1, D), const2), pl.BlockSpec((1, D), const2)],
        out_specs=pl.BlockSpec((tm, D), lambda i: (i, 0)),
        out_shape=jax.ShapeDtypeStruct((T, D), F32),
        compiler_params=pltpu.CompilerParams(dimension_semantics=("arbitrary",),
                                             vmem_limit_bytes=V7X_VMEM_LIMIT_BYTES),
        name="swiglu_ffn",
    )(h1, wg_bf, wu_bf, wd_bf, g, b)


def _expert_kernel(be_ref, nact_ref, tok_ref, tok_next_ref, row_ref, gate_ref, h1_hbm, wg_ref, wu_ref, wd_ref,
                   out_hbm, xbuf, ybuf, sem_in, sem_out, *, blk, fc):
    del be_ref
    i = pl.program_id(0)
    n_active = nact_ref[0]
    slot = i % 2

    n_tiles = xbuf.shape[1] // blk
    tile = lambda n: pl.ds(pl.multiple_of(n * n_tiles, n_tiles), n_tiles)

    def row_in(tok, s, r):
        return pltpu.make_async_copy(h1_hbm.at[tile(tok), :], xbuf.at[s, tile(r), :], sem_in.at[s])

    def row_out(dst, s, r):
        return pltpu.make_async_copy(ybuf.at[s, tile(r), :], out_hbm.at[tile(dst), :], sem_out.at[s])

    def for_rows(fn):
        def step(r, carry):
            fn(r)
            return carry
        lax.fori_loop(0, blk, step, 0, unroll=8)

    @pl.when(jnp.logical_and(i == 0, n_active > 0))
    def _():
        for_rows(lambda r: row_in(tok_ref[0, 0, r], 0, r).start())

    @pl.when(i + 1 < n_active)
    def _():
        for_rows(lambda r: row_in(tok_next_ref[0, 0, r], 1 - slot, r).start())

    @pl.when(i < n_active)
    def _():
        for_rows(lambda r: row_in(0, slot, r).wait())

        @pl.when(i >= 2)
        def _():
            for_rows(lambda r: row_out(0, slot, r).wait())

        x = jnp.concatenate([xbuf[slot, pl.ds(j, blk, stride=n_tiles), :] for j in range(n_tiles)], axis=1)
        y = _swiglu(x.astype(BF16), wg_ref, wu_ref, wd_ref, fc) * gate_ref[...]
        for j in range(n_tiles):
            ybuf[slot, pl.ds(j, blk, stride=n_tiles), :] = y[:, j * LANES:(j + 1) * LANES]

        @pl.when(i == 0)
        def _():
            spare = out_hbm.shape[0] - 2 * blk * n_tiles
            for p in range(2):
                fill = pltpu.make_async_copy(ybuf.at[0], out_hbm.at[pl.ds(spare + p * blk * n_tiles, blk * n_tiles), :],
                                             sem_out.at[0])
                fill.start()
                fill.wait()

        for_rows(lambda r: row_out(row_ref[0, 0, r], slot, r).start())

        @pl.when(i == n_active - 1)
        def _():
            for_rows(lambda r: row_out(0, slot, r).wait())

            @pl.when(i >= 1)
            def _():
                for_rows(lambda r: row_out(0, 1 - slot, r).wait())


def _combine_kernel(h1_ref, y2_ref, g_ref, b_ref, out_ref, *, alpha):
    tm, d = h1_ref.shape
    n_tiles = d // LANES
    cols = [sum(y2_ref[pl.ds(k * n_tiles + j, tm, stride=TOP_K * n_tiles), :] for k in range(TOP_K))
            for j in range(n_tiles)]
    out_ref[...] = _layer_norm(alpha * h1_ref[...] + jnp.concatenate(cols, axis=1), g_ref[...], b_ref[...])


def _moe_ffn(h1, route, wg_bf, wu_bf, wd_bf, g, b, *, alpha, blk=512, fc=256, tm=512):
    T, D = h1.shape
    E, _, FF = wg_bf.shape
    A = T * TOP_K
    assert A % blk == 0 and FF % fc == 0 and T % tm == 0 and D % LANES == 0
    n_tiles = D // LANES
    n_slots = A + E * blk
    n_blocks = n_slots // blk
    i32 = jnp.int32

    flat_e = route[:, 0:TOP_K].astype(i32).reshape(-1)
    flat_g = route[:, TOP_K:2 * TOP_K].reshape(-1)
    order = jnp.argsort(flat_e, stable=True).astype(i32)
    counts = jnp.sum((flat_e[:, None] == jnp.arange(E, dtype=i32)[None, :]).astype(i32), axis=0)
    padded = ((counts + blk - 1) // blk) * blk
    pend = jnp.cumsum(padded)
    pstart = pend - padded
    start = jnp.cumsum(counts) - counts
    n_active = (pend[-1] // blk).astype(i32)
    blk_id = jnp.arange(n_blocks, dtype=i32)
    block_e = jnp.minimum(jnp.searchsorted(pend, blk_id * blk, side='right'), E - 1).astype(i32)
    block_e = jnp.where(blk_id < n_active, block_e, block_e[jnp.maximum(n_active - 1, 0)])
    s = jnp.arange(n_slots, dtype=i32)
    e_s = block_e[s // blk]
    rank = s - pstart[e_s]
    valid = jnp.logical_and(rank < counts[e_s], s < pend[-1])
    a = order[jnp.clip(start[e_s] + rank, 0, A - 1)]
    slot_tok = jnp.where(valid, a // TOP_K, 0).astype(i32).reshape(n_blocks, 1, blk)
    spare_row = A + ((s // blk) % 2) * blk + s % blk
    slot_row = jnp.where(valid, a, spare_row).astype(i32).reshape(n_blocks, 1, blk)
    slot_gate = jnp.where(valid, flat_g[a], 0.0).astype(F32).reshape(n_slots, 1)

    smem_blk = lambda imap: pl.BlockSpec((1, 1, blk), imap, memory_space=pltpu.SMEM)
    wspec = lambda shape: pl.BlockSpec(shape, lambda i, be, na: (be[i], 0, 0))
    y2 = pl.pallas_call(
        functools.partial(_expert_kernel, blk=blk, fc=fc),
        grid_spec=pltpu.PrefetchScalarGridSpec(
            num_scalar_prefetch=2,
            grid=(n_blocks,),
            in_specs=[smem_blk(lambda i, be, na: (i, 0, 0)),
                      smem_blk(lambda i, be, na: (jnp.minimum(i + 1, n_blocks - 1), 0, 0)),
                      smem_blk(lambda i, be, na: (i, 0, 0)),
                      pl.BlockSpec((blk, 1), lambda i, be, na: (i, 0)),
                      pl.BlockSpec(memory_space=pl.ANY),
                      wspec((1, D, FF)), wspec((1, D, FF)), wspec((1, FF, D))],
            out_specs=pl.BlockSpec(memory_space=pl.ANY),
            scratch_shapes=[pltpu.VMEM((2, blk * n_tiles, LANES), F32), pltpu.VMEM((2, blk * n_tiles, LANES), F32),
                            pltpu.SemaphoreType.DMA((2,)), pltpu.SemaphoreType.DMA((2,))]),
        out_shape=jax.ShapeDtypeStruct(((A + 2 * blk) * n_tiles, LANES), F32),
        compiler_params=pltpu.CompilerParams(dimension_semantics=("arbitrary",),
                                             vmem_limit_bytes=V7X_VMEM_LIMIT_BYTES),
        name="expert_ffn",
    )(block_e, n_active.reshape(1), slot_tok, slot_tok, slot_row, slot_gate, h1.reshape(T * n_tiles, LANES),
      wg_bf, wu_bf, wd_bf)

    const2 = lambda i: (0, 0)
    return pl.pallas_call(
        functools.partial(_combine_kernel, alpha=alpha),
        grid=(T // tm,),
        in_specs=[pl.BlockSpec((tm, D), lambda i: (i, 0)),
                  pl.BlockSpec((tm * TOP_K * n_tiles, LANES), lambda i: (i, 0)),
                  pl.BlockSpec((1, D), const2), pl.BlockSpec((1, D), const2)],
        out_specs=pl.BlockSpec((tm, D), lambda i: (i, 0)),
        out_shape=jax.ShapeDtypeStruct((T, D), F32),
        compiler_params=pltpu.CompilerParams(dimension_semantics=("arbitrary",),
                                             vmem_limit_bytes=V7X_VMEM_LIMIT_BYTES),
        name="moe_combine_ln",
    )(h1, y2, g, b)


def _split_router(router):
    D, E = router.shape
    r = jnp.zeros((D, LANES), F32).at[:, :E].set(router.astype(F32))
    hi = r.astype(BF16)
    lo = (r - hi.astype(F32)).astype(BF16)
    return hi, lo, E


def kernel(x, ln_in_g, ln_in_b, w_in, pool_w, pool_scale, w_out, ln1_g, ln1_b, ffn_wg, ffn_wu, ffn_wd,
           moe_router, moe_wg, moe_wu, moe_wd, ln2_g, ln2_b):
    B, S, D = x.shape
    depth = w_in.shape[0]
    T = B * S
    alpha = float((2 * depth) ** 0.25)
    row = lambda v: v.reshape(1, -1).astype(F32)
    h = x.astype(F32)
    for layer in range(depth):
        h, qkv, yp = _inproj(h, row(ln_in_g), row(ln_in_b), w_in[layer].astype(BF16),
                             pool_w[layer].astype(BF16), row(pool_scale[layer]), apply_ln=(layer == 0))
        ys = _attention(qkv)
        i = layer // 2
        router_split = _split_router(moe_router[i]) if layer % 2 else None
        h1, route = _post(h.reshape(T, D), yp.reshape(T, -1), ys.reshape(T, -1), w_out[layer].astype(BF16),
                          row(ln1_g[layer]), row(ln1_b[layer]), router_split, alpha=alpha)
        if layer % 2 == 0:
            h = _ffn(h1, ffn_wg[i][None].astype(BF16), ffn_wu[i][None].astype(BF16), ffn_wd[i][None].astype(BF16),
                     row(ln2_g[layer]), row(ln2_b[layer]), alpha=alpha)
        else:
            h = _moe_ffn(h1, route, moe_wg[i].astype(BF16), moe_wu[i].astype(BF16), moe_wd[i].astype(BF16),
                         row(ln2_g[layer]), row(ln2_b[layer]), alpha=alpha)
        h = h.reshape(B, S, D)
    return h.astype(x.dtype)
```

```python
import functools
import math

import jax
import jax.numpy as jnp
from jax import lax
from jax.experimental import pallas as pl
from jax.experimental.pallas import tpu as pltpu

F32 = jnp.float32
BF16 = jnp.bfloat16

LN_EPS = 1e-5
POOL_WINDOWS = (2, 4, 8, 16)
HEAD_DIM = 64
TOP_K = 2
LANES = 128
POOL_HALO = 16
V7X_VMEM_LIMIT_BYTES = 56 * 1024 * 1024

ATT_TK = 256
ATT_TQ = 2 * ATT_TK
MASKED_SCORE = -1e30
NEG_LOG2_E = -1.4426950408889634


def _layer_norm(x, g, b):
    mu = jnp.mean(x, axis=-1, keepdims=True)
    xc = x - mu
    var = jnp.mean(xc * xc, axis=-1, keepdims=True)
    return xc * lax.rsqrt(var + LN_EPS) * g + b


def _inproj_kernel(x_ref, lng_ref, lnb_ref, w_ref, pw_ref, ps_ref, *rest, apply_ln, tm, pool_width):
    if apply_ln:
        h_ref, qkv_ref, yp_ref, halo_ref = rest
    else:
        qkv_ref, yp_ref, halo_ref = rest
    si = pl.program_id(1)
    x = x_ref[0]
    if apply_ln:
        h = _layer_norm(x, lng_ref[...], lnb_ref[...])
        h_ref[0] = h
    else:
        h = x
    proj = jnp.dot(h.astype(BF16), w_ref[...], preferred_element_type=F32)
    u = proj[:, :pool_width]

    @pl.when(si == 0)
    def _():
        halo_ref[...] = jnp.zeros_like(halo_ref)

    ext = jnp.concatenate([halo_ref[...], u], axis=0)
    halo_ref[...] = u[tm - POOL_HALO:, :]
    pos = si * tm + lax.broadcasted_iota(jnp.int32, (tm, 1), 0)
    group = pool_width // len(POOL_WINDOWS)
    for gi, w in enumerate(POOL_WINDOWS):
        lo, hi = gi * group, (gi + 1) * group
        s = ext[:, lo:hi]
        shift = 1
        while shift < w:
            s = s + pltpu.roll(s, shift, 0)
            shift *= 2
        cnt = jnp.minimum(pos + 1, w).astype(F32)
        d = s[POOL_HALO:, :] / cnt - u[:, lo:hi]
        y = jnp.dot(d.astype(BF16), pw_ref[gi], preferred_element_type=F32)
        yp_ref[0, :, lo:hi] = (y * ps_ref[:, lo:hi]).astype(BF16)
    sbw = (proj.shape[1] - pool_width) // 3
    q = proj[:, pool_width:pool_width + sbw] * (1.0 / math.sqrt(HEAD_DIM))
    qkv_ref[0, :, :sbw] = q.astype(BF16)
    qkv_ref[0, :, sbw:] = proj[:, pool_width + sbw:].astype(BF16)


def _inproj(x, lng, lnb, w_bf, pw_bf, ps, *, apply_ln, tm=512):
    B, S, D = x.shape
    n_out = w_bf.shape[1]
    pool_width = ps.shape[1]
    qkv_w = n_out - pool_width
    assert S % tm == 0 and tm >= POOL_HALO
    out_shape = [jax.ShapeDtypeStruct((B, S, qkv_w), BF16), jax.ShapeDtypeStruct((B, S, pool_width), BF16)]
    out_specs = [pl.BlockSpec((1, tm, qkv_w), lambda b, s: (b, s, 0)),
                 pl.BlockSpec((1, tm, pool_width), lambda b, s: (b, s, 0))]
    if apply_ln:
        out_shape = [jax.ShapeDtypeStruct((B, S, D), F32)] + out_shape
        out_specs = [pl.BlockSpec((1, tm, D), lambda b, s: (b, s, 0))] + out_specs
    const2 = lambda b, s: (0, 0)
    res = pl.pallas_call(
        functools.partial(_inproj_kernel, apply_ln=apply_ln, tm=tm, pool_width=pool_width),
        grid=(B, S // tm),
        in_specs=[pl.BlockSpec((1, tm, D), lambda b, s: (b, s, 0)),
                  pl.BlockSpec((1, D), const2), pl.BlockSpec((1, D), const2),
                  pl.BlockSpec((D, n_out), const2),
                  pl.BlockSpec(pw_bf.shape, lambda b, s: (0, 0, 0)),
                  pl.BlockSpec((1, pool_width), const2)],
        out_specs=out_specs,
        out_shape=out_shape,
        scratch_shapes=[pltpu.VMEM((POOL_HALO, pool_width), F32)],
        compiler_params=pltpu.CompilerParams(dimension_semantics=("arbitrary", "arbitrary"),
                                             vmem_limit_bytes=V7X_VMEM_LIMIT_BYTES),
        name="inproj_pool",
    )(x, lng, lnb, w_bf, pw_bf, ps)
    if apply_ln:
        return res[0], res[1], res[2]
    return x, res[0], res[1]


def _attn_kernel(q_ref, k_ref, v_ref, tri_ref, o_ref, acc_ref, run_ref, *, tq, tk):
    qi = pl.program_id(2)
    lane = lax.broadcasted_iota(jnp.int32, (tq, LANES), 1)
    first_head = lane < HEAD_DIM
    q = q_ref[0]
    zero = jnp.zeros_like(q)
    q_heads = (jnp.where(first_head, q, zero), jnp.where(first_head, zero, q))
    acc_ref[...] = jnp.zeros_like(acc_ref)
    run_ref[...] = jnp.zeros_like(run_ref)
    tri = tri_ref[...]

    def scores(h, kstart, causal_mask):
        kb = k_ref[0, pl.ds(kstart, tk), :]
        z = lax.dot_general(q_heads[h], kb, (((1,), (1,)), ((), ())), preferred_element_type=F32)
        if causal_mask is not None:
            z = jnp.where(causal_mask, z, MASKED_SCORE)
        sp = jnp.maximum(z, 0.0) + jnp.log(1.0 + jnp.exp2(jnp.abs(z) * NEG_LOG2_E))
        later = jnp.dot(sp.astype(BF16), tri, preferred_element_type=F32)
        return z - sp, later, later[:, 0:1] + sp[:, 0:1]

    def span(kstart, masks):
        nblk = len(masks)
        wide = lambda r: jnp.concatenate([r] * (tk // LANES), axis=1)
        for h in range(2):
            parts = [scores(h, kstart + b * tk, masks[b]) for b in range(nblk)]
            run = run_ref[h]
            att = [None] * nblk
            for b in reversed(range(nblk)):
                log_beta, later, total = parts[b]
                att[b] = jnp.exp(log_beta - later - wide(run))
                run = run + jnp.broadcast_to(total, (tq, LANES))
            run_ref[h] = run
            acc_ref[h] += jnp.dot(jnp.concatenate(att, axis=1).astype(BF16),
                                  v_ref[0, pl.ds(kstart, nblk * tk), :], preferred_element_type=F32)

    lead = (lax.broadcasted_iota(jnp.int32, (tq, tk), 0) - lax.broadcasted_iota(jnp.int32, (tq, tk), 1))
    diag_masks = (lead > 0, lead > tk)
    odd = qi % 2

    @pl.when(odd == 0)
    def _():
        span(pl.multiple_of(qi * tq, tq), diag_masks)

    @pl.when(odd == 1)
    def _():
        span(pl.multiple_of((qi - 1) * tq, tq), (None, None) + diag_masks)

    top = qi - 1 - odd

    def body(it, carry):
        span(pl.multiple_of((top - 2 * it - 1) * tq, tq), (None,) * 4)
        return carry

    lax.fori_loop(0, qi // 2, body, 0)
    o_ref[0] = jnp.where(first_head, acc_ref[0], acc_ref[1]).astype(o_ref.dtype)


def _attention(qkv, *, tq=ATT_TQ, tk=ATT_TK):
    B, S, W = qkv.shape
    sbw = W // 3
    n_pairs = sbw // LANES
    assert tq == 2 * tk and S % tq == 0 and sbw % LANES == 0 and LANES == 2 * HEAD_DIM
    idx = jnp.arange(tk)
    tri = (idx[:, None] > idx[None, :]).astype(BF16)
    return pl.pallas_call(
        functools.partial(_attn_kernel, tq=tq, tk=tk),
        grid=(B, n_pairs, S // tq),
        in_specs=[pl.BlockSpec((1, tq, LANES), lambda b, p, i: (b, i, p)),
                  pl.BlockSpec((1, S, LANES), lambda b, p, i: (b, 0, n_pairs + p)),
                  pl.BlockSpec((1, S, LANES), lambda b, p, i: (b, 0, 2 * n_pairs + p)),
                  pl.BlockSpec((tk, tk), lambda b, p, i: (0, 0))],
        out_specs=pl.BlockSpec((1, tq, LANES), lambda b, p, i: (b, i, p)),
        out_shape=jax.ShapeDtypeStruct((B, S, sbw), BF16),
        scratch_shapes=[pltpu.VMEM((2, tq, LANES), F32), pltpu.VMEM((2, tq, LANES), F32)],
        compiler_params=pltpu.CompilerParams(dimension_semantics=("arbitrary", "arbitrary", "arbitrary"),
                                             vmem_limit_bytes=V7X_VMEM_LIMIT_BYTES),
        name="stickbreak_attn",
    )(qkv, qkv, qkv, tri)


def _post_kernel(h_ref, yp_ref, ys_ref, wo_ref, g_ref, b_ref, *rest, alpha, n_experts):
    if n_experts:
        rhi_ref, rlo_ref, h1_ref, route_ref, h1_tiles_ref = rest
    else:
        (h1_ref,) = rest
    pw = yp_ref.shape[1]
    mix = (jnp.dot(yp_ref[...], wo_ref[:pw, :], preferred_element_type=F32)
           + jnp.dot(ys_ref[...], wo_ref[pw:, :], preferred_element_type=F32))
    h1 = _layer_norm(alpha * h_ref[...] + mix, g_ref[...], b_ref[...])
    h1_ref[...] = h1
    if n_experts:
        n_tiles = h1.shape[1] // LANES
        for j in range(n_tiles):
            h1_tiles_ref[pl.ds(j, h1.shape[0], stride=n_tiles), :] = h1[:, j * LANES:(j + 1) * LANES]
        x_hi = h1.astype(BF16)
        x_lo = (h1 - x_hi.astype(F32)).astype(BF16)
        logits = (jnp.dot(x_hi, rhi_ref[...], preferred_element_type=F32)
                  + jnp.dot(x_hi, rlo_ref[...], preferred_element_type=F32)
                  + jnp.dot(x_lo, rhi_ref[...], preferred_element_type=F32))
        lane = lax.broadcasted_iota(jnp.int32, logits.shape, 1)
        neg = jnp.float32(-jnp.inf)
        l1 = jnp.where(lane < n_experts, logits, neg)
        m1 = jnp.max(l1, axis=1, keepdims=True)
        i1 = jnp.min(jnp.where(l1 == m1, lane, LANES), axis=1, keepdims=True)
        l2 = jnp.where(lane == i1, neg, l1)
        m2 = jnp.max(l2, axis=1, keepdims=True)
        i2 = jnp.min(jnp.where(l2 == m2, lane, LANES), axis=1, keepdims=True)
        e2 = jnp.exp(m2 - m1)
        den = 1.0 + e2
        g1 = 1.0 / den
        g2 = e2 / den
        route_ref[...] = jnp.where(lane == 0, i1.astype(F32),
                                   jnp.where(lane == 1, i2.astype(F32),
                                             jnp.where(lane == 2, g1, jnp.where(lane == 3, g2, 0.0))))


def _post(h, yp, ys, wo_bf, g, b, router_split, *, alpha, tm=512):
    T, D = h.shape
    pw, sw = yp.shape[1], ys.shape[1]
    n_experts = 0 if router_split is None else router_split[2]
    const2 = lambda i: (0, 0)
    in_specs = [pl.BlockSpec((tm, D), lambda i: (i, 0)),
                pl.BlockSpec((tm, pw), lambda i: (i, 0)),
                pl.BlockSpec((tm, sw), lambda i: (i, 0)),
                pl.BlockSpec((pw + sw, D), const2),
                pl.BlockSpec((1, D), const2), pl.BlockSpec((1, D), const2)]
    args = [h, yp, ys, wo_bf, g, b]
    out_shape = [jax.ShapeDtypeStruct((T, D), F32)]
    out_specs = [pl.BlockSpec((tm, D), lambda i: (i, 0))]
    if n_experts:
        in_specs += [pl.BlockSpec((D, LANES), const2), pl.BlockSpec((D, LANES), const2)]
        args += [router_split[0], router_split[1]]
        n_tiles = D // LANES
        out_shape += [jax.ShapeDtypeStruct((T, LANES), F32), jax.ShapeDtypeStruct((T * n_tiles, LANES), F32)]
        out_specs += [pl.BlockSpec((tm, LANES), lambda i: (i, 0)),
                      pl.BlockSpec((tm * n_tiles, LANES), lambda i: (i, 0))]
    res = pl.pallas_call(
        functools.partial(_post_kernel, alpha=alpha, n_experts=n_experts),
        grid=(T // tm,),
        in_specs=in_specs, out_specs=out_specs, out_shape=out_shape,
        compiler_params=pltpu.CompilerParams(dimension_semantics=("arbitrary",),
                                             vmem_limit_bytes=V7X_VMEM_LIMIT_BYTES),
        name="outproj_ln",
    )(*args)
    return tuple(res) if n_experts else (res[0], None, None)


def _swiglu(x_bf, wg_ref, wu_ref, wd_ref, fc):
    ff = wg_ref.shape[2]
    y = None
    for c in range(ff // fc):
        gcol = jnp.dot(x_bf, wg_ref[0, :, c * fc:(c + 1) * fc], preferred_element_type=F32)
        ucol = jnp.dot(x_bf, wu_ref[0, :, c * fc:(c + 1) * fc], preferred_element_type=F32)
        a = (gcol * jax.nn.sigmoid(gcol)) * ucol
        part = jnp.dot(a.astype(BF16), wd_ref[0, c * fc:(c + 1) * fc, :], preferred_element_type=F32)
        y = part if y is None else y + part
    return y


def _ffn_kernel(h1_ref, wg_ref, wu_ref, wd_ref, g_ref, b_ref, out_ref, *, alpha, fc):
    h1 = h1_ref[...]
    y = _swiglu(h1.astype(BF16), wg_ref, wu_ref, wd_ref, fc)
    out_ref[...] = _layer_norm(alpha * h1 + y, g_ref[...], b_ref[...])


def _ffn(h1, wg_bf, wu_bf, wd_bf, g, b, *, alpha, tm=512, fc=256):
    T, D = h1.shape
    _, _, FF = wg_bf.shape
    assert FF % fc == 0 and T % tm == 0
    const2 = lambda i: (0, 0)
    const3 = lambda i: (0, 0, 0)
    return pl.pallas_call(
        functools.partial(_ffn_kernel, alpha=alpha, fc=fc),
        grid=(T // tm,),
        in_specs=[pl.BlockSpec((tm, D), lambda i: (i, 0)),
                  pl.BlockSpec((1, D, FF), const3), pl.BlockSpec((1, D, FF), const3),
                  pl.BlockSpec((1, FF, D), const3),
                  pl.BlockSpec((1, D), const2), pl.BlockSpec((1, D), const2)],
        out_specs=pl.BlockSpec((tm, D), lambda i: (i, 0)),
        out_shape=jax.ShapeDtypeStruct((T, D), F32),
        compiler_params=pltpu.CompilerParams(dimension_semantics=("arbitrary",),
                                             vmem_limit_bytes=V7X_VMEM_LIMIT_BYTES),
        name="swiglu_ffn",
    )(h1, wg_bf, wu_bf, wd_bf, g, b)


def _expert_kernel(be_ref, nact_ref, tok_ref, tok_next_ref, row_ref, gate_ref, h1_hbm, wg_ref, wu_ref, wd_ref,
                   out_hbm, xbuf, ybuf, sem_in, sem_out, *, blk, fc):
    del be_ref
    i = pl.program_id(0)
    n_active = nact_ref[0]
    slot = i % 2

    n_tiles = xbuf.shape[1] // blk
    tile = lambda n: pl.ds(pl.multiple_of(n * n_tiles, n_tiles), n_tiles)

    def row_in(tok, s, r):
        return pltpu.make_async_copy(h1_hbm.at[tile(tok), :], xbuf.at[s, tile(r), :], sem_in.at[s])

    def row_out(dst, s, r):
        return pltpu.make_async_copy(ybuf.at[s, tile(r), :], out_hbm.at[tile(dst), :], sem_out.at[s])

    def for_rows(fn):
        def step(r, carry):
            fn(r)
            return carry
        lax.fori_loop(0, blk, step, 0, unroll=8)

    @pl.when(jnp.logical_and(i == 0, n_active > 0))
    def _():
        for_rows(lambda r: row_in(tok_ref[0, 0, r], 0, r).start())

    @pl.when(i + 1 < n_active)
    def _():
        for_rows(lambda r: row_in(tok_next_ref[0, 0, r], 1 - slot, r).start())

    @pl.when(i < n_active)
    def _():
        for_rows(lambda r: row_in(0, slot, r).wait())

        @pl.when(i >= 2)
        def _():
            for_rows(lambda r: row_out(0, slot, r).wait())

        x = jnp.concatenate([xbuf[slot, pl.ds(j, blk, stride=n_tiles), :] for j in range(n_tiles)], axis=1)
        y = _swiglu(x.astype(BF16), wg_ref, wu_ref, wd_ref, fc) * gate_ref[...]
        for j in range(n_tiles):
            ybuf[slot, pl.ds(j, blk, stride=n_tiles), :] = y[:, j * LANES:(j + 1) * LANES]

        @pl.when(i == 0)
        def _():
            spare = out_hbm.shape[0] - 2 * blk * n_tiles
            for p in range(2):
                fill = pltpu.make_async_copy(ybuf.at[0], out_hbm.at[pl.ds(spare + p * blk * n_tiles, blk * n_tiles), :],
                                             sem_out.at[0])
                fill.start()
                fill.wait()

        for_rows(lambda r: row_out(row_ref[0, 0, r], slot, r).start())

        @pl.when(i == n_active - 1)
        def _():
            for_rows(lambda r: row_out(0, slot, r).wait())

            @pl.when(i >= 1)
            def _():
                for_rows(lambda r: row_out(0, 1 - slot, r).wait())


def _combine_kernel(h1_ref, y2_ref, g_ref, b_ref, out_ref, *, alpha):
    tm, d = h1_ref.shape
    n_tiles = d // LANES
    cols = [sum(y2_ref[pl.ds(k * n_tiles + j, tm, stride=TOP_K * n_tiles), :] for k in range(TOP_K))
            for j in range(n_tiles)]
    out_ref[...] = _layer_norm(alpha * h1_ref[...] + jnp.concatenate(cols, axis=1), g_ref[...], b_ref[...])


def _moe_ffn(h1, h1_tiles, route, wg_bf, wu_bf, wd_bf, g, b, *, alpha, blk=512, fc=256, tm=512):
    T, D = h1.shape
    E, _, FF = wg_bf.shape
    A = T * TOP_K
    assert A % blk == 0 and FF % fc == 0 and T % tm == 0 and D % LANES == 0
    n_tiles = D // LANES
    n_slots = A + E * blk
    n_blocks = n_slots // blk
    i32 = jnp.int32

    flat_e = route[:, 0:TOP_K].astype(i32).reshape(-1)
    flat_g = route[:, TOP_K:2 * TOP_K].reshape(-1)
    _, sorted_a, sorted_g = lax.sort((flat_e, jnp.arange(A, dtype=i32), flat_g), num_keys=1, is_stable=True)
    experts = jnp.arange(E, dtype=i32)
    counts = jnp.sum((flat_e[None, :] == experts[:, None]).astype(i32), axis=1)
    padded = ((counts + blk - 1) // blk) * blk
    pend = jnp.cumsum(padded)
    pstart = pend - padded
    start = jnp.cumsum(counts) - counts
    n_active = (pend[-1] // blk).astype(i32)
    blk_id = jnp.arange(n_blocks, dtype=i32)
    block_e = jnp.minimum(jnp.sum((pend[None, :] <= (blk_id * blk)[:, None]).astype(i32), axis=1), E - 1)
    block_e = jnp.where(blk_id < n_active, block_e, block_e[jnp.maximum(n_active - 1, 0)])
    first_rank = blk_id * blk - pstart[block_e]
    n_valid = jnp.where(blk_id < n_active, jnp.clip(counts[block_e] - first_rank, 0, blk), 0)
    src0 = jnp.where(n_valid > 0, start[block_e] + first_rank, 0)
    take = lambda v: jax.vmap(lambda st: lax.dynamic_slice(jnp.concatenate([v, jnp.zeros((blk,), v.dtype)]),
                                                           (st,), (blk,)))(src0)
    blk_a, blk_g = take(sorted_a), take(sorted_g)
    r = jnp.arange(blk, dtype=i32)[None, :]
    valid = r < n_valid[:, None]
    slot_tok = jnp.where(valid, blk_a // TOP_K, 0).astype(i32).reshape(n_blocks, 1, blk)
    spare_row = A + (blk_id % 2)[:, None] * blk + r
    slot_row = jnp.where(valid, blk_a, spare_row).astype(i32).reshape(n_blocks, 1, blk)
    slot_gate = jnp.where(valid, blk_g, 0.0).astype(F32).reshape(n_slots, 1)

    smem_blk = lambda imap: pl.BlockSpec((1, 1, blk), imap, memory_space=pltpu.SMEM)
    wspec = lambda shape: pl.BlockSpec(shape, lambda i, be, na: (be[i], 0, 0))
    y2 = pl.pallas_call(
        functools.partial(_expert_kernel, blk=blk, fc=fc),
        grid_spec=pltpu.PrefetchScalarGridSpec(
            num_scalar_prefetch=2,
            grid=(n_blocks,),
            in_specs=[smem_blk(lambda i, be, na: (i, 0, 0)),
                      smem_blk(lambda i, be, na: (jnp.minimum(i + 1, n_blocks - 1), 0, 0)),
                      smem_blk(lambda i, be, na: (i, 0, 0)),
                      pl.BlockSpec((blk, 1), lambda i, be, na: (i, 0)),
                      pl.BlockSpec(memory_space=pl.ANY),
                      wspec((1, D, FF)), wspec((1, D, FF)), wspec((1, FF, D))],
            out_specs=pl.BlockSpec(memory_space=pl.ANY),
            scratch_shapes=[pltpu.VMEM((2, blk * n_tiles, LANES), F32), pltpu.VMEM((2, blk * n_tiles, LANES), F32),
                            pltpu.SemaphoreType.DMA((2,)), pltpu.SemaphoreType.DMA((2,))]),
        out_shape=jax.ShapeDtypeStruct(((A + 2 * blk) * n_tiles, LANES), F32),
        compiler_params=pltpu.CompilerParams(dimension_semantics=("arbitrary",),
                                             vmem_limit_bytes=V7X_VMEM_LIMIT_BYTES),
        name="expert_ffn",
    )(block_e, n_active.reshape(1), slot_tok, slot_tok, slot_row, slot_gate, h1_tiles, wg_bf, wu_bf, wd_bf)

    const2 = lambda i: (0, 0)
    return pl.pallas_call(
        functools.partial(_combine_kernel, alpha=alpha),
        grid=(T // tm,),
        in_specs=[pl.BlockSpec((tm, D), lambda i: (i, 0)),
                  pl.BlockSpec((tm * TOP_K * n_tiles, LANES), lambda i: (i, 0)),
                  pl.BlockSpec((1, D), const2), pl.BlockSpec((1, D), const2)],
        out_specs=pl.BlockSpec((tm, D), lambda i: (i, 0)),
        out_shape=jax.ShapeDtypeStruct((T, D), F32),
        compiler_params=pltpu.CompilerParams(dimension_semantics=("arbitrary",),
                                             vmem_limit_bytes=V7X_VMEM_LIMIT_BYTES),
        name="moe_combine_ln",
    )(h1, y2, g, b)


def _split_router(router):
    D, E = router.shape
    r = jnp.zeros((D, LANES), F32).at[:, :E].set(router.astype(F32))
    hi = r.astype(BF16)
    lo = (r - hi.astype(F32)).astype(BF16)
    return hi, lo, E


def kernel(x, ln_in_g, ln_in_b, w_in, pool_w, pool_scale, w_out, ln1_g, ln1_b, ffn_wg, ffn_wu, ffn_wd,
           moe_router, moe_wg, moe_wu, moe_wd, ln2_g, ln2_b):
    B, S, D = x.shape
    depth = w_in.shape[0]
    T = B * S
    alpha = float((2 * depth) ** 0.25)
    row = lambda v: v.reshape(1, -1).astype(F32)
    h = x.astype(F32)
    for layer in range(depth):
        h, qkv, yp = _inproj(h, row(ln_in_g), row(ln_in_b), w_in[layer].astype(BF16),
                             pool_w[layer].astype(BF16), row(pool_scale[layer]), apply_ln=(layer == 0))
        ys = _attention(qkv)
        i = layer // 2
        router_split = _split_router(moe_router[i]) if layer % 2 else None
        h1, route, h1_tiles = _post(h.reshape(T, D), yp.reshape(T, -1), ys.reshape(T, -1),
                                    w_out[layer].astype(BF16), row(ln1_g[layer]), row(ln1_b[layer]),
                                    router_split, alpha=alpha)
        if layer % 2 == 0:
            h = _ffn(h1, ffn_wg[i][None].astype(BF16), ffn_wu[i][None].astype(BF16), ffn_wd[i][None].astype(BF16),
                     row(ln2_g[layer]), row(ln2_b[layer]), alpha=alpha)
        else:
            h = _moe_ffn(h1, h1_tiles, route, moe_wg[i].astype(BF16), moe_wu[i].astype(BF16), moe_wd[i].astype(BF16),
                         row(ln2_g[layer]), row(ln2_b[layer]), alpha=alpha)
        h = h.reshape(B, S, D)
    return h.astype(x.dtype)
```
